```python
import math, functools
import jax, jax.numpy as jnp
from jax import lax
import numpy as np

D_MODEL = 1024
BATCH = 8
SEQ = 2048
DEPTH = 1
DEC_BATCH = 8
DEC_SEQ = 16
PAST_LEN = 2048

CHUNK = 64
SSM_INNER = 2 * D_MODEL
SSM_HEAD_DIM = 64
SSM_HEADS = SSM_INNER // SSM_HEAD_DIM
SSM_GROUPS = 8
SSM_STATE = 128
CONV_K = 4
SSM_XBC = SSM_INNER + 2 * SSM_GROUPS * SSM_STATE
MLSTM_INNER = D_MODEL
MLSTM_HEADS = 4
MLSTM_HEAD_DIM = MLSTM_INNER // MLSTM_HEADS
N_EXPERTS = 32
TOP_K = 4
D_FF = D_MODEL
SWIGLU_LIMIT = 7.0
SWIGLU_ALPHA = 1.702
MOE_BLOCK = 128
DEEPNORM_ALPHA = (2 * DEPTH) ** 0.25
DEEPNORM_BETA = (8 * DEPTH) ** -0.25
NORM_EPS = 1e-5
IN_SIZES = (SSM_INNER, SSM_XBC, SSM_HEADS, MLSTM_INNER, MLSTM_INNER, MLSTM_INNER, MLSTM_INNER, MLSTM_HEADS, MLSTM_HEADS, D_MODEL, D_MODEL)
IN_TOTAL = sum(IN_SIZES)

kernel_name = 'hybrid_ssd_mlstm_moe_stream_step'


def _split_cols(u):
    idx, acc = [], 0
    for s in IN_SIZES[:-1]:
        acc += s
        idx.append(acc)
    return jnp.split(u, idx, axis=-1)


def _layer_norm(x, g, b):
    xf = x.astype(jnp.float32)
    mu = jnp.mean(xf, -1, keepdims=True)
    xc = xf - mu
    var = jnp.mean(xc * xc, -1, keepdims=True)
    return (xc * lax.rsqrt(var + NORM_EPS) * g.astype(jnp.float32) + b.astype(jnp.float32)).astype(x.dtype)


def _grouped_rms(y, n_groups):
    shp = y.shape
    yg = y.reshape(shp[:-1] + (n_groups, shp[-1] // n_groups))
    yg = yg * lax.rsqrt(jnp.mean(yg * yg, -1, keepdims=True) + NORM_EPS)
    return yg.reshape(shp)


def _causal_conv(u, hist, w, b):
    up = jnp.concatenate([hist.astype(u.dtype), u], axis=1)
    seq_len = u.shape[1]
    out = b
    for j in range(CONV_K):
        out = out + up[:, j:j + seq_len] * w[j]
    return out, up[:, -(CONV_K - 1):]


def _chunked(step, carry, seqs, cl):
    bsz, seq_len = seqs[0].shape[:2]
    nc = seq_len // cl
    to_c = lambda a: jnp.swapaxes(a.reshape((bsz, nc, cl) + a.shape[2:]), 0, 1)
    carry, ys = lax.scan(step, carry, tuple(to_c(a) for a in seqs))
    ys = jnp.swapaxes(ys, 0, 1)
    return carry, ys.reshape((bsz, seq_len) + ys.shape[3:])


def _ssd_step(a, s_prev, inp):
    xs, bm, cm, dt = inp
    bsz, c = xs.shape[:2]
    g, j = SSM_GROUPS, SSM_HEADS // SSM_GROUPS
    cum = jnp.cumsum(dt * a, axis=1)
    causal = jnp.tril(jnp.ones((c, c), bool))
    seg = cum[:, :, None, :] - cum[:, None, :, :]
    decay = jnp.exp(jnp.where(causal[None, :, :, None], seg, -jnp.inf))
    cb = jnp.einsum('btgn,bsgn->btsg', cm, bm)
    w = decay.reshape(bsz, c, c, g, j) * cb[..., None] * dt.reshape(bsz, 1, c, g, j)
    x5 = xs.reshape(bsz, c, g, j, SSM_HEAD_DIM)
    s5 = s_prev.reshape(bsz, g, j, SSM_HEAD_DIM, SSM_STATE)
    y = jnp.einsum('btsgj,bsgjp->btgjp', w, x5)
    y = y + jnp.einsum('btgn,bgjpn->btgjp', cm, s5) * jnp.exp(cum).reshape(bsz, c, g, j, 1)
    tail = (dt * jnp.exp(cum[:, -1:, :] - cum)).reshape(bsz, c, g, j)
    s_new = jnp.exp(cum[:, -1]).reshape(bsz, g, j, 1, 1) * s5 + jnp.einsum('bsgn,bsgj,bsgjp->bgjpn', bm, tail, x5)
    return s_new.reshape(s_prev.shape), y.reshape(bsz, c, SSM_HEADS, SSM_HEAD_DIM)


def _mlstm_step(carry, inp):
    c_prev, n_prev, m_prev = carry
    q, k, v, ig, lf = inp
    c = q.shape[1]
    bcum = jnp.cumsum(lf, axis=1)
    causal = jnp.tril(jnp.ones((c, c), bool))
    dmat = jnp.where(causal[None, :, :, None], bcum[:, :, None, :] - bcum[:, None, :, :] + ig[:, None, :, :], -jnp.inf)
    inter = bcum + m_prev[:, None, :]
    m_t = jnp.maximum(inter, jnp.max(dmat, axis=2))
    wts = jnp.exp(dmat - m_t[:, :, None, :])
    w_in = jnp.exp(inter - m_t)
    s = jnp.einsum('bthd,bshd->btsh', q, k) * wts
    num = jnp.einsum('btsh,bshe->bthe', s, v) + w_in[..., None] * jnp.einsum('bhed,bthd->bthe', c_prev, q)
    den = jnp.sum(s, axis=2) + w_in * jnp.einsum('bhd,bthd->bth', n_prev, q)
    h = num / jnp.maximum(jnp.abs(den), jnp.exp(-m_t))[..., None]
    m_new = m_t[:, -1]
    w_tail = jnp.exp(bcum[:, -1:, :] - bcum + ig - m_new[:, None, :])
    keep = jnp.exp(bcum[:, -1] + m_prev - m_new)
    c_new = keep[..., None, None] * c_prev + jnp.einsum('bsh,bshe,bshd->bhed', w_tail, v, k)
    n_new = keep[..., None] * n_prev + jnp.einsum('bsh,bshd->bhd', w_tail, k)
    return (c_new, n_new, m_new), h


def _token_mix(x, hist_a, ssm0, hist_b, c0, n0, m0, p):
    f32 = jnp.float32
    dty = x.dtype
    bsz, seq_len, _ = x.shape
    cl = min(CHUNK, seq_len)
    u = x @ p['w_in'] + p['b_in']
    z, xbc, dt_raw, q, k, v, o, ig, fg, ga, gb = _split_cols(u)
    xbc, new_hist_a = _causal_conv(xbc, hist_a, p['conv_a_w'], p['conv_a_b'])
    xbc = jax.nn.silu(xbc).astype(f32)
    xs, b_ssm, c_ssm = jnp.split(xbc, [SSM_INNER, SSM_INNER + SSM_GROUPS * SSM_STATE], axis=-1)
    xs = xs.reshape(bsz, seq_len, SSM_HEADS, SSM_HEAD_DIM)
    b_ssm = b_ssm.reshape(bsz, seq_len, SSM_GROUPS, SSM_STATE)
    c_ssm = c_ssm.reshape(bsz, seq_len, SSM_GROUPS, SSM_STATE)
    dt = jax.nn.softplus(dt_raw.astype(f32))
    a = -jnp.exp(p['a_log'].astype(f32))
    ssm, y_a = _chunked(functools.partial(_ssd_step, a), ssm0.astype(f32), (xs, b_ssm, c_ssm, dt), cl)
    y_a = y_a + p['d_skip'].astype(f32)[:, None] * xs
    y_a = y_a.reshape(bsz, seq_len, SSM_INNER) * jax.nn.silu(z.astype(f32))
    y_a = _grouped_rms(y_a, SSM_GROUPS) * p['norm_a_w'].astype(f32)
    qk, new_hist_b = _causal_conv(jnp.concatenate([q, k], axis=-1), hist_b, p['conv_b_w'], p['conv_b_b'])
    qk = jax.nn.silu(qk).astype(f32)
    hshape = (bsz, seq_len, MLSTM_HEADS, MLSTM_HEAD_DIM)
    qh = qk[..., :MLSTM_INNER].reshape(hshape)
    kh = qk[..., MLSTM_INNER:].reshape(hshape) * (MLSTM_HEAD_DIM ** -0.5)
    vh = v.astype(f32).reshape(hshape)
    lf = jax.nn.log_sigmoid(fg.astype(f32))
    (c_mem, n_mem, m_mem), h = _chunked(_mlstm_step, (c0.astype(f32), n0.astype(f32), m0.astype(f32)), (qh, kh, vh, ig.astype(f32), lf), cl)
    h = _grouped_rms(h.reshape(bsz, seq_len, MLSTM_INNER), MLSTM_HEADS) * p['norm_b_w'].astype(f32)
    y_b = jax.nn.sigmoid(o.astype(f32)) * h
    merged = jax.nn.sigmoid(ga) * (y_a.astype(dty) @ p['w_proj_a']) + jax.nn.sigmoid(gb) * (y_b.astype(dty) @ p['w_proj_b'])
    out = merged @ p['w_out']
    return out, (new_hist_a, ssm.astype(dty), new_hist_b, c_mem.astype(dty), n_mem.astype(dty), m_mem.astype(dty))


def _moe(x2d, p):
    n_tok = x2d.shape[0]
    dty = x2d.dtype
    logits = (x2d @ p['w_router'] + p['b_router']).astype(jnp.float32)
    top_v, top_i = lax.top_k(logits, TOP_K)
    gates = jax.nn.softmax(top_v, axis=-1)
    m = n_tok * TOP_K
    flat_e = top_i.reshape(-1)
    flat_t = jnp.repeat(jnp.arange(n_tok, dtype=jnp.int32), TOP_K)
    flat_w = gates.reshape(-1).astype(dty)
    order = jnp.argsort(flat_e)
    se = flat_e[order]
    counts = jnp.bincount(flat_e, length=N_EXPERTS)
    padded = (counts + MOE_BLOCK - 1) // MOE_BLOCK * MOE_BLOCK
    starts = jnp.cumsum(counts) - counts
    pends = jnp.cumsum(padded)
    pstarts = pends - padded
    dest = pstarts[se] + jnp.arange(m) - starts[se]
    n_blocks = -(-m // MOE_BLOCK) + N_EXPERTS
    rows = n_blocks * MOE_BLOCK
    row_tok = jnp.full((rows,), n_tok, jnp.int32).at[dest].set(flat_t[order])
    row_w = jnp.zeros((rows,), dty).at[dest].set(flat_w[order])
    block_e = jnp.minimum(jnp.searchsorted(pends, jnp.arange(n_blocks) * MOE_BLOCK, side='right'), N_EXPERTS - 1)
    xpad = jnp.concatenate([x2d, jnp.zeros((1, x2d.shape[1]), dty)], axis=0)
    xb = xpad[row_tok].reshape(n_blocks, MOE_BLOCK, x2d.shape[1])
    w_gu, b_gu, w_down, b_down = p['w_gu'], p['b_gu'], p['w_down'], p['b_down']

    def expert_block(args):
        xblk, e = args
        hgu = xblk @ w_gu[e] + b_gu[e]
        gate = jnp.minimum(hgu[:, :D_FF], SWIGLU_LIMIT)
        up = jnp.clip(hgu[:, D_FF:], -SWIGLU_LIMIT, SWIGLU_LIMIT)
        act = (up + 1.0) * gate * jax.nn.sigmoid(SWIGLU_ALPHA * gate)
        return act @ w_down[e] + b_down[e]

    out = lax.map(expert_block, (xb, block_e)).reshape(rows, x2d.shape[1])
    y = jnp.zeros((n_tok + 1, x2d.shape[1]), dty).at[row_tok].add(out * row_w[:, None])
    return y[:n_tok]


def _layer(x, states, p):
    mix, new_states = _token_mix(x, *states, p)
    h = _layer_norm(DEEPNORM_ALPHA * x + mix, p['ln1_g'], p['ln1_b'])
    bsz, seq_len, d = h.shape
    ffn = _moe(h.reshape(bsz * seq_len, d), p).reshape(bsz, seq_len, d)
    y = _layer_norm(DEEPNORM_ALPHA * h + ffn, p['ln2_g'], p['ln2_b'])
    return y, new_states


def setup_inputs(seed: int = 0) -> dict:
    key = jax.random.key(seed)
    ks = iter(jax.random.split(key, 48))
    nrm = lambda shape, scale: scale * jax.random.normal(next(ks), shape, jnp.float32)
    L = DEPTH
    D = D_MODEL
    x_prompt = nrm((BATCH, SEQ, D), 1.0)
    x_sample = nrm((DEC_BATCH, DEC_SEQ, D), 1.0)
    state_conv_a = nrm((L, DEC_BATCH, CONV_K - 1, SSM_XBC), 1.0)
    state_ssm = nrm((L, DEC_BATCH, SSM_HEADS, SSM_HEAD_DIM, SSM_STATE), 0.1)
    state_conv_b = nrm((L, DEC_BATCH, CONV_K - 1, 2 * MLSTM_INNER), 1.0)
    state_mlstm_c = nrm((L, DEC_BATCH, MLSTM_HEADS, MLSTM_HEAD_DIM, MLSTM_HEAD_DIM), 0.1)
    state_mlstm_n = nrm((L, DEC_BATCH, MLSTM_HEADS, MLSTM_HEAD_DIM), 0.1)
    state_mlstm_m = nrm((L, DEC_BATCH, MLSTM_HEADS), 1.0)
    w_in = nrm((L, D, IN_TOTAL), D ** -0.5)
    dt0 = jnp.exp(jax.random.uniform(next(ks), (L, SSM_HEADS), jnp.float32, math.log(1e-3), math.log(1e-1)))
    dt_bias = dt0 + jnp.log(-jnp.expm1(-dt0))
    f_bias = jnp.linspace(3.0, 6.0, MLSTM_HEADS, dtype=jnp.float32)[None, :] + nrm((L, MLSTM_HEADS), 0.01)
    b_in = jnp.concatenate([nrm((L, SSM_INNER + SSM_XBC), 0.01), dt_bias, nrm((L, 4 * MLSTM_INNER), 0.01), nrm((L, MLSTM_HEADS), 0.1), f_bias, nrm((L, 2 * D), 0.01)], axis=-1)
    conv_a_w = nrm((L, CONV_K, SSM_XBC), CONV_K ** -0.5)
    conv_a_b = nrm((L, SSM_XBC), 0.01)
    a_log = jnp.log(jax.random.uniform(next(ks), (L, SSM_HEADS), jnp.float32, 1.0, 16.0))
    d_skip = 1.0 + nrm((L, SSM_HEADS), 0.01)
    norm_a_w = 1.0 + nrm((L, SSM_INNER), 0.01)
    conv_b_w = nrm((L, CONV_K, 2 * MLSTM_INNER), CONV_K ** -0.5)
    conv_b_b = nrm((L, 2 * MLSTM_INNER), 0.01)
    norm_b_w = 1.0 + nrm((L, MLSTM_INNER), 0.01)
    w_proj_a = nrm((L, SSM_INNER, D), SSM_INNER ** -0.5)
    w_proj_b = nrm((L, MLSTM_INNER, D), MLSTM_INNER ** -0.5)
    w_out = nrm((L, D, D), D ** -0.5 * DEEPNORM_BETA)
    ln1_g = 1.0 + nrm((L, D), 0.01)
    ln1_b = nrm((L, D), 0.01)
    w_router = nrm((L, D, N_EXPERTS), D ** -0.5)
    b_router = nrm((L, N_EXPERTS), 0.01)
    w_gu = nrm((L, N_EXPERTS, D, 2 * D_FF), D ** -0.5)
    b_gu = nrm((L, N_EXPERTS, 2 * D_FF), 0.01)
    w_down = nrm((L, N_EXPERTS, D_FF, D), D_FF ** -0.5 * DEEPNORM_BETA)
    b_down = nrm((L, N_EXPERTS, D), 0.01)
    ln2_g = 1.0 + nrm((L, D), 0.01)
    ln2_b = nrm((L, D), 0.01)
    return {'x_prompt': x_prompt, 'x_sample': x_sample,
            'state_conv_a': state_conv_a, 'state_ssm': state_ssm, 'state_conv_b': state_conv_b,
            'state_mlstm_c': state_mlstm_c, 'state_mlstm_n': state_mlstm_n, 'state_mlstm_m': state_mlstm_m,
            'w_in': w_in, 'b_in': b_in, 'conv_a_w': conv_a_w, 'conv_a_b': conv_a_b, 'a_log': a_log,
            'd_skip': d_skip, 'norm_a_w': norm_a_w, 'conv_b_w': conv_b_w, 'conv_b_b': conv_b_b,
            'norm_b_w': norm_b_w, 'w_proj_a': w_proj_a, 'w_proj_b': w_proj_b, 'w_out': w_out,
            'ln1_g': ln1_g, 'ln1_b': ln1_b, 'w_router': w_router, 'b_router': b_router,
            'w_gu': w_gu, 'b_gu': b_gu, 'w_down': w_down, 'b_down': b_down,
            'ln2_g': ln2_g, 'ln2_b': ln2_b}


def reference(x_prompt, x_sample, state_conv_a, state_ssm, state_conv_b, state_mlstm_c, state_mlstm_n, state_mlstm_m,
              w_in, b_in, conv_a_w, conv_a_b, a_log, d_skip, norm_a_w, conv_b_w, conv_b_b, norm_b_w,
              w_proj_a, w_proj_b, w_out, ln1_g, ln1_b, w_router, b_router, w_gu, b_gu, w_down, b_down,
              ln2_g, ln2_b):
    y_prompt, y_sample = x_prompt, x_sample
    dty = x_prompt.dtype
    bsz = x_prompt.shape[0]
    prompt_states, sample_states = [], []
    for l in range(DEPTH):
        p = {'w_in': w_in[l], 'b_in': b_in[l], 'conv_a_w': conv_a_w[l], 'conv_a_b': conv_a_b[l],
             'a_log': a_log[l], 'd_skip': d_skip[l], 'norm_a_w': norm_a_w[l], 'conv_b_w': conv_b_w[l],
             'conv_b_b': conv_b_b[l], 'norm_b_w': norm_b_w[l], 'w_proj_a': w_proj_a[l], 'w_proj_b': w_proj_b[l],
             'w_out': w_out[l], 'ln1_g': ln1_g[l], 'ln1_b': ln1_b[l], 'w_router': w_router[l],
             'b_router': b_router[l], 'w_gu': w_gu[l], 'b_gu': b_gu[l], 'w_down': w_down[l],
             'b_down': b_down[l], 'ln2_g': ln2_g[l], 'ln2_b': ln2_b[l]}
        zero_states = (jnp.zeros((bsz, CONV_K - 1, SSM_XBC), dty),
                       jnp.zeros((bsz, SSM_HEADS, SSM_HEAD_DIM, SSM_STATE), dty),
                       jnp.zeros((bsz, CONV_K - 1, 2 * MLSTM_INNER), dty),
                       jnp.zeros((bsz, MLSTM_HEADS, MLSTM_HEAD_DIM, MLSTM_HEAD_DIM), dty),
                       jnp.zeros((bsz, MLSTM_HEADS, MLSTM_HEAD_DIM), dty),
                       jnp.zeros((bsz, MLSTM_HEADS), dty))
        y_prompt, sp = _layer(y_prompt, zero_states, p)
        past = (state_conv_a[l], state_ssm[l], state_conv_b[l], state_mlstm_c[l], state_mlstm_n[l], state_mlstm_m[l])
        y_sample, ss = _layer(y_sample, past, p)
        prompt_states.append(sp)
        sample_states.append(ss)
    p_conv_a = jnp.stack([s[0] for s in prompt_states])
    p_ssm = jnp.stack([s[1] for s in prompt_states])
    p_conv_b = jnp.stack([s[2] for s in prompt_states])
    p_mlstm_c = jnp.stack([s[3] for s in prompt_states])
    p_mlstm_n = jnp.stack([s[4] for s in prompt_states])
    p_mlstm_m = jnp.stack([s[5] for s in prompt_states])
    s_conv_a = jnp.stack([s[0] for s in sample_states])
    s_ssm = jnp.stack([s[1] for s in sample_states])
    s_conv_b = jnp.stack([s[2] for s in sample_states])
    s_mlstm_c = jnp.stack([s[3] for s in sample_states])
    s_mlstm_n = jnp.stack([s[4] for s in sample_states])
    s_mlstm_m = jnp.stack([s[5] for s in sample_states])
    return (y_prompt, y_sample, p_conv_a, p_ssm, p_conv_b, p_mlstm_c, p_mlstm_n, p_mlstm_m,
            s_conv_a, s_ssm, s_conv_b, s_mlstm_c, s_mlstm_n, s_mlstm_m)
```

```python
import functools

import jax
import jax.numpy as jnp
from jax import lax
from jax.experimental import pallas as pl
from jax.experimental.pallas import tpu as pltpu

F32 = jnp.float32
BF16 = jnp.bfloat16

D_MODEL = 1024
CHUNK = 64
SSM_INNER = 2 * D_MODEL
SSM_P = 64
SSM_H = SSM_INNER // SSM_P
SSM_G = 8
SSM_J = SSM_H // SSM_G
SSM_N = 128
CONV_K = 4
SSM_XBC = SSM_INNER + 2 * SSM_G * SSM_N
ML_INNER = D_MODEL
ML_H = 4
ML_D = ML_INNER // ML_H
N_EXPERTS = 32
TOP_K = 4
D_FF = D_MODEL
SWIGLU_LIMIT = 7.0
SWIGLU_ALPHA = 1.702
MOE_BLOCK = 128
DEEPNORM_ALPHA = 2.0 ** 0.25
NORM_EPS = 1e-5
IN_SIZES = (SSM_INNER, SSM_XBC, SSM_H, ML_INNER, ML_INNER, ML_INNER, ML_INNER, ML_H, ML_H, D_MODEL, D_MODEL)

LANES = 128
GW = SSM_J * SSM_P
U_BIG = 12 * D_MODEL
IG_LANE = SSM_H
FG_LANE = SSM_H + ML_H
NEG_BIG = -1e30
VMEM_LIMIT = 56 * 1024 * 1024


def _sigmoid(x):
    return 1.0 / (1.0 + jnp.exp(-x))


def _softplus(x):
    return jnp.maximum(x, 0.0) + jnp.log(1.0 + jnp.exp(-jnp.abs(x)))


def _prefix_sum_rows(x):
    n = x.shape[0]
    row = lax.broadcasted_iota(jnp.int32, x.shape, 0)
    k = 1
    while k < n:
        x = x + jnp.where(row >= k, pltpu.roll(x, k, 0), 0.0)
        k *= 2
    return x


def _split3(x):
    hi = x.astype(BF16)
    r1 = x - hi.astype(F32)
    mid = r1.astype(BF16)
    lo = (r1 - mid.astype(F32)).astype(BF16)
    return hi, mid, lo


def _inproj_kernel(x_ref, w_ref, b_ref, o_ref):
    x = x_ref[...].astype(BF16)
    o_ref[...] = jnp.dot(x, w_ref[...], preferred_element_type=F32) + b_ref[...]


def _inproj(x2d, w_big, b_big, tm=512, tn=2048):
    m = x2d.shape[0]
    return pl.pallas_call(
        _inproj_kernel,
        out_shape=jax.ShapeDtypeStruct((m, U_BIG), F32),
        grid=(U_BIG // tn, m // tm),
        in_specs=[pl.BlockSpec((tm, D_MODEL), lambda j, i: (i, 0)),
                  pl.BlockSpec((D_MODEL, tn), lambda j, i: (0, j)),
                  pl.BlockSpec((1, tn), lambda j, i: (0, j))],
        out_specs=pl.BlockSpec((tm, tn), lambda j, i: (i, j)),
        compiler_params=pltpu.CompilerParams(
            dimension_semantics=("arbitrary", "arbitrary"), vmem_limit_bytes=VMEM_LIMIT),
        name="inproj",
    )(x2d, w_big, b_big)


def _small_kernel(x_ref, w_ref, b_ref, o_ref):
    o_ref[...] = jnp.dot(x_ref[...], w_ref[...], preferred_element_type=F32,
                         precision=lax.Precision.HIGHEST) + b_ref[...]


def _small_proj(x2d, w_small, b_small, tm=512):
    m = x2d.shape[0]
    return pl.pallas_call(
        _small_kernel,
        out_shape=jax.ShapeDtypeStruct((m, LANES), F32),
        grid=(m // tm,),
        in_specs=[pl.BlockSpec((tm, D_MODEL), lambda i: (i, 0)),
                  pl.BlockSpec((D_MODEL, LANES), lambda i: (0, 0)),
                  pl.BlockSpec((1, LANES), lambda i: (0, 0))],
        out_specs=pl.BlockSpec((tm, LANES), lambda i: (i, 0)),
        compiler_params=pltpu.CompilerParams(dimension_semantics=("arbitrary",)),
        name="small_proj",
    )(x2d, w_small, b_small)


def _conv_chunk(u_ref, hist_ref, cw_ref, cb_ref, ext_ref, first):
    @pl.when(first)
    def _():
        ext_ref[0:8, :] = hist_ref[...]

    ext_ref[8:8 + CHUNK, :] = u_ref[...]
    acc = cb_ref[...]
    for j in range(CONV_K):
        lo = 8 - (CONV_K - 1) + j
        acc = acc + ext_ref[lo:lo + CHUNK, :] * cw_ref[j:j + 1, :]
    ext_ref[0:8, :] = ext_ref[CHUNK:CHUNK + 8, :]
    return acc


def _ssd_kernel(valid_len, xbc_ref, z_ref, sm_ref, hist_ref, s0_ref, cw_ref, cb_ref, alog_ref,
                dskip_ref, nw_ref, exp_ref, eye_ref, causal_ref, bd_ref,
                ya_ref, sout_ref, ext_ref, st_ref):
    c = pl.program_id(1)
    nc = pl.num_programs(1)

    @pl.when(c == 0)
    def _():
        st_ref[...] = s0_ref[...].T

    act = _conv_chunk(xbc_ref, hist_ref, cw_ref, cb_ref, ext_ref, c == 0)
    act = act * _sigmoid(act)
    xs = act[:, :SSM_INNER]
    bm_f = act[:, SSM_INNER:SSM_INNER + SSM_G * SSM_N]
    bm = bm_f.astype(BF16)
    cm = act[:, SSM_INNER + SSM_G * SSM_N:].astype(BF16)

    dt = _softplus(sm_ref[...])
    if valid_len < CHUNK:
        row = lax.broadcasted_iota(jnp.int32, dt.shape, 0)
        dt = jnp.where(row < valid_len, dt, 0.0)
    a = -jnp.exp(alog_ref[...])
    cum = _prefix_sum_rows(dt * a)

    pieces = _split3(cum) + _split3(dt)
    stacked = jnp.concatenate(pieces, axis=0)
    r = jnp.dot(stacked, exp_ref[...], preferred_element_type=F32)
    cum_x = r[0:CHUNK] + r[CHUNK:2 * CHUNK] + r[2 * CHUNK:3 * CHUNK]
    dt_x = r[3 * CHUNK:4 * CHUNK] + r[4 * CHUNK:5 * CHUNK] + r[5 * CHUNK:6 * CHUNK]

    eye = eye_ref[...]
    cum_row = jnp.sum(cum_x * eye, axis=0, keepdims=True)
    dt_row = jnp.sum(dt_x * eye, axis=0, keepdims=True)
    cum_last = cum_x[CHUNK - 1:CHUNK, :]

    decay = jnp.where(causal_ref[...] > 0.0, jnp.exp(cum_x - cum_row), 0.0)
    expc_x = jnp.exp(cum_x)
    tail_x = dt_x * jnp.exp(cum_last - cum_x)
    dec_x = jnp.exp(cum_last)

    xt = (xs * tail_x).astype(BF16)
    xs_b = xs.astype(BF16)
    bdmask = bd_ref[...] > 0.0
    zrow_b = jnp.zeros((CHUNK, SSM_N), F32)
    zrow_x = jnp.zeros((CHUNK, GW), BF16)
    ys = []
    for g in range(SSM_G):
        gs = slice(g * GW, (g + 1) * GW)
        ns = slice(g * SSM_N, (g + 1) * SSM_N)
        cg = cm[:, ns]
        bg = bm[:, ns]
        btile = jnp.concatenate([bg] * SSM_J, axis=0)
        cb = lax.dot_general(cg, btile, (((1,), (1,)), ((), ())), preferred_element_type=F32)
        w = (decay[:, gs] * cb * dt_row[:, gs]).astype(BF16)
        xg = xs_b[:, gs]
        bd = jnp.where(bdmask, jnp.concatenate([xg] * SSM_J, axis=0), jnp.zeros((), BF16))
        y = jnp.dot(w, bd, preferred_element_type=F32)
        s_g = st_ref[:, gs]
        y = y + jnp.dot(cg, s_g.astype(BF16), preferred_element_type=F32) * expc_x[:, gs]
        ys.append(y)
        bgt = jnp.concatenate([bm_f[:, ns], zrow_b], axis=0).T.astype(BF16)
        xtg = jnp.concatenate([xt[:, gs], zrow_x], axis=0)
        st_ref[:, gs] = dec_x[:, gs] * s_g + jnp.dot(bgt, xtg, preferred_element_type=F32)

    y = jnp.concatenate(ys, axis=1) + dskip_ref[...] * xs
    zz = z_ref[...]
    y = y * (zz * _sigmoid(zz))
    outs = []
    for g in range(SSM_G):
        yg = y[:, g * GW:(g + 1) * GW]
        outs.append(yg * lax.rsqrt(jnp.mean(yg * yg, axis=-1, keepdims=True) + NORM_EPS))
    ya_ref[...] = (jnp.concatenate(outs, axis=1) * nw_ref[...]).astype(ya_ref.dtype)

    @pl.when(c == nc - 1)
    def _():
        sout_ref[...] = st_ref[...].T


def _ssd_consts():
    lane = jnp.arange(SSM_INNER)
    expand = (jnp.arange(LANES)[:, None] == (lane // SSM_P)[None, :]).astype(BF16)
    t = jnp.arange(CHUNK)[:, None]
    s = (lane % SSM_P)[None, :]
    eye = (t == s).astype(F32)
    causal = (s <= t).astype(F32)
    r = jnp.arange(GW)
    bd = ((r[:, None] // SSM_P) == (r[None, :] // SSM_P)).astype(F32)
    return expand, eye, causal, bd


def _ssd(u_big, u_small, hist8, s0, cw, cb, alog, dskip_x, nw, valid_len):
    bsz, seq, _ = u_big.shape
    nc = seq // CHUNK
    expand, eye, causal, bd = _ssd_consts()
    const = lambda shape: pl.BlockSpec(shape, lambda b, c: (0,) * len(shape))
    return pl.pallas_call(
        functools.partial(_ssd_kernel, valid_len),
        out_shape=(jax.ShapeDtypeStruct((bsz, seq, SSM_INNER), BF16),
                   jax.ShapeDtypeStruct((bsz, SSM_H * SSM_P, SSM_N), F32)),
        grid=(bsz, nc),
        in_specs=[pl.BlockSpec((None, CHUNK, SSM_XBC), lambda b, c: (b, c, 0)),
                  pl.BlockSpec((None, CHUNK, SSM_INNER), lambda b, c: (b, c, 2)),
                  pl.BlockSpec((None, CHUNK, LANES), lambda b, c: (b, c, 0)),
                  pl.BlockSpec((None, 8, SSM_XBC), lambda b, c: (b, 0, 0)),
                  pl.BlockSpec((None, SSM_H * SSM_P, SSM_N), lambda b, c: (b, 0, 0)),
                  const((CONV_K, SSM_XBC)), const((1, SSM_XBC)), const((1, LANES)),
                  const((1, SSM_INNER)), const((1, SSM_INNER)),
                  const((LANES, SSM_INNER)), const((CHUNK, SSM_INNER)), const((CHUNK, SSM_INNER)),
                  const((GW, GW))],
        out_specs=(pl.BlockSpec((None, CHUNK, SSM_INNER), lambda b, c: (b, c, 0)),
                   pl.BlockSpec((None, SSM_H * SSM_P, SSM_N), lambda b, c: (b, 0, 0))),
        scratch_shapes=[pltpu.VMEM((CHUNK + 8, SSM_XBC), F32),
                        pltpu.VMEM((SSM_N, SSM_H * SSM_P), F32)],
        compiler_params=pltpu.CompilerParams(
            dimension_semantics=("arbitrary", "arbitrary"), vmem_limit_bytes=VMEM_LIMIT),
        name="ssd",
    )(u_big, u_big, u_small, hist8, s0, cw, cb, alog, dskip_x, nw, expand, eye, causal, bd)


def _mlstm_kernel(valid_len, qk_ref, v_ref, o_ref, sm_ref, hist_ref, c0_ref, n0_ref, m0_ref,
                  cw_ref, cb_ref, nw_ref,
                  yb_ref, cout_ref, nout_ref, mout_ref, ext_ref, c_ref, n_ref, m_ref):
    c = pl.program_id(1)
    nc = pl.num_programs(1)

    @pl.when(c == 0)
    def _():
        c_ref[...] = c0_ref[...]
        n_ref[...] = n0_ref[...]
        m_ref[...] = m0_ref[...]

    act = _conv_chunk(qk_ref, hist_ref, cw_ref, cb_ref, ext_ref, c == 0)
    act = act * _sigmoid(act)
    vv = v_ref[...]

    sm = sm_ref[...]
    lf = -_softplus(-sm)
    row = lax.broadcasted_iota(jnp.int32, sm.shape, 0)
    lane = lax.broadcasted_iota(jnp.int32, sm.shape, 1)
    if valid_len < CHUNK:
        lf = jnp.where(row < valid_len, lf, 0.0)
        sm = jnp.where(row < valid_len, sm, NEG_BIG)
    bcum = _prefix_sum_rows(lf)
    is_f = (lane >= FG_LANE) & (lane < FG_LANE + ML_H)
    both = jnp.where(is_f, bcum, sm)
    both_t = jnp.concatenate([both, jnp.zeros_like(both)], axis=0).T

    tri = lax.broadcasted_iota(jnp.int32, (CHUNK, CHUNK), 1) <= lax.broadcasted_iota(jnp.int32, (CHUNK, CHUNK), 0)
    lane_row = lax.broadcasted_iota(jnp.int32, (1, LANES), 1)
    m_all = m_ref[...]
    m_next = m_all
    zrow = jnp.zeros((CHUNK, ML_D), BF16)
    zrow_f = jnp.zeros((CHUNK, ML_D), F32)
    hs = []
    for h in range(ML_H):
        ds = slice(h * ML_D, (h + 1) * ML_D)
        q = act[:, ds]
        k = act[:, ML_INNER + h * ML_D:ML_INNER + (h + 1) * ML_D] * (ML_D ** -0.5)
        v = vv[:, ds]
        qb, kb = q.astype(BF16), k.astype(BF16)
        bc = bcum[:, FG_LANE + h:FG_LANE + h + 1]
        igc = sm[:, IG_LANE + h:IG_LANE + h + 1]
        bct = both_t[FG_LANE + h:FG_LANE + h + 1, 0:CHUNK]
        igt = both_t[IG_LANE + h:IG_LANE + h + 1, 0:CHUNK]
        m_prev = m_all[:, h:h + 1]
        dmat = jnp.where(tri, bc - bct + igt, -jnp.inf)
        inter = bc + m_prev
        m_t = jnp.maximum(inter, jnp.max(dmat, axis=1, keepdims=True))
        wts = jnp.exp(dmat - m_t)
        w_in = jnp.exp(inter - m_t)
        s = lax.dot_general(qb, kb, (((1,), (1,)), ((), ())), preferred_element_type=F32) * wts
        c_prev = c_ref[h]
        n_prev = n_ref[h:h + 1, :]
        qc = lax.dot_general(qb, c_prev.astype(BF16), (((1,), (1,)), ((), ())), preferred_element_type=F32)
        num = jnp.dot(s.astype(BF16), v.astype(BF16), preferred_element_type=F32) + w_in * qc
        den = jnp.sum(s, axis=1, keepdims=True) + w_in * jnp.sum(q * n_prev, axis=1, keepdims=True)
        hh = num / jnp.maximum(jnp.abs(den), jnp.exp(-m_t))
        m_new = m_t[CHUNK - 1:CHUNK, :]
        bc_last = bc[CHUNK - 1:CHUNK, :]
        w_tail = jnp.exp(bc_last - bc + igc - m_new)
        keep = jnp.exp(bc_last + m_prev - m_new)
        vw_t = jnp.concatenate([v * w_tail, zrow_f], axis=0).T.astype(BF16)
        kpad = jnp.concatenate([kb, zrow], axis=0)
        c_ref[h] = keep * c_prev + jnp.dot(vw_t, kpad, preferred_element_type=F32)
        n_ref[h:h + 1, :] = keep * n_prev + jnp.sum(k * w_tail, axis=0, keepdims=True)
        m_next = jnp.where(lane_row == h, m_new, m_next)
        hs.append(hh * lax.rsqrt(jnp.mean(hh * hh, axis=-1, keepdims=True) + NORM_EPS))
    m_ref[...] = m_next
    hn = jnp.concatenate(hs, axis=1) * nw_ref[...]
    yb_ref[...] = (_sigmoid(o_ref[...]) * hn).astype(yb_ref.dtype)

    @pl.when(c == nc - 1)
    def _():
        cout_ref[...] = c_ref[...]
        nout_ref[...] = n_ref[...]
        mout_ref[...] = m_ref[...]


def _mlstm(u_big, u_small, hist8, c0, n0, m0p, cw, cb, nw, valid_len):
    bsz, seq, _ = u_big.shape
    nc = seq // CHUNK
    const = lambda shape: pl.BlockSpec(shape, lambda b, c: (0,) * len(shape))
    return pl.pallas_call(
        functools.partial(_mlstm_kernel, valid_len),
        out_shape=(jax.ShapeDtypeStruct((bsz, seq, ML_INNER), BF16),
                   jax.ShapeDtypeStruct((bsz, ML_H, ML_D, ML_D), F32),
                   jax.ShapeDtypeStruct((bsz, ML_H, ML_D), F32),
                   jax.ShapeDtypeStruct((bsz, 1, LANES), F32)),
        grid=(bsz, nc),
        in_specs=[pl.BlockSpec((None, CHUNK, 2 * ML_INNER), lambda b, c: (b, c, 3)),
                  pl.BlockSpec((None, CHUNK, ML_INNER), lambda b, c: (b, c, 8)),
                  pl.BlockSpec((None, CHUNK, ML_INNER), lambda b, c: (b, c, 9)),
                  pl.BlockSpec((None, CHUNK, LANES), lambda b, c: (b, c, 0)),
                  pl.BlockSpec((None, 8, 2 * ML_INNER), lambda b, c: (b, 0, 0)),
                  pl.BlockSpec((None, ML_H, ML_D, ML_D), lambda b, c: (b, 0, 0, 0)),
                  pl.BlockSpec((None, ML_H, ML_D), lambda b, c: (b, 0, 0)),
                  pl.BlockSpec((None, 1, LANES), lambda b, c: (b, 0, 0)),
                  const((CONV_K, 2 * ML_INNER)), const((1, 2 * ML_INNER)), const((1, ML_INNER))],
        out_specs=(pl.BlockSpec((None, CHUNK, ML_INNER), lambda b, c: (b, c, 0)),
                   pl.BlockSpec((None, ML_H, ML_D, ML_D), lambda b, c: (b, 0, 0, 0)),
                   pl.BlockSpec((None, ML_H, ML_D), lambda b, c: (b, 0, 0)),
                   pl.BlockSpec((None, 1, LANES), lambda b, c: (b, 0, 0))),
        scratch_shapes=[pltpu.VMEM((CHUNK + 8, 2 * ML_INNER), F32),
                        pltpu.VMEM((ML_H, ML_D, ML_D), F32),
                        pltpu.VMEM((ML_H, ML_D), F32),
                        pltpu.VMEM((1, LANES), F32)],
        compiler_params=pltpu.CompilerParams(
            dimension_semantics=("arbitrary", "arbitrary"), vmem_limit_bytes=VMEM_LIMIT),
        name="mlstm",
    )(u_big, u_big, u_big, u_small, hist8, c0, n0, m0p, cw, cb, nw)


def _layer_norm(r, g, b):
    mu = jnp.mean(r, axis=-1, keepdims=True)
    rc = r - mu
    var = jnp.mean(rc * rc, axis=-1, keepdims=True)
    return rc * lax.rsqrt(var + NORM_EPS) * g + b


def _mix_kernel(ya_ref, yb_ref, ga_ref, gb_ref, x_ref, wpa_ref, wpb_ref, wout_ref, g_ref, b_ref,
                wr_ref, br_ref, h_ref, ti_ref, tw_ref):
    pa = jnp.dot(ya_ref[...], wpa_ref[...], preferred_element_type=F32)
    pb = jnp.dot(yb_ref[...], wpb_ref[...], preferred_element_type=F32)
    merged = _sigmoid(ga_ref[...]) * pa + _sigmoid(gb_ref[...]) * pb
    out = jnp.dot(merged.astype(BF16), wout_ref[...], preferred_element_type=F32)
    h = _layer_norm(DEEPNORM_ALPHA * x_ref[...] + out, g_ref[...], b_ref[...])
    h_ref[...] = h

    logits = jnp.dot(h, wr_ref[...], preferred_element_type=F32,
                     precision=lax.Precision.HIGHEST) + br_ref[...]
    lane = lax.broadcasted_iota(jnp.int32, logits.shape, 1)
    lane_f = lane.astype(F32)
    vals, idxs = [], []
    cur = logits
    for _ in range(TOP_K):
        m = jnp.max(cur, axis=1, keepdims=True)
        idx = jnp.min(jnp.where(cur == m, lane_f, float(LANES)), axis=1, keepdims=True)
        vals.append(m)
        idxs.append(idx)
        cur = jnp.where(lane_f == idx, -jnp.inf, cur)
    es = [jnp.exp(v - vals[0]) for v in vals]
    tot = es[0] + es[1] + es[2] + es[3]
    ti = jnp.zeros(logits.shape, F32)
    tw = jnp.zeros(logits.shape, F32)
    for k in range(TOP_K):
        ti = jnp.where(lane == k, idxs[k], ti)
        tw = jnp.where(lane == k, es[k] / tot, tw)
    ti_ref[...] = ti.astype(jnp.int32)
    tw_ref[...] = tw


def _mix(ya, yb, u_big, x2d, wpa, wpb, wout, g1, b1, wr, br, tm=256):
    m = x2d.shape[0]
    tm = min(tm, m)
    const = lambda shape: pl.BlockSpec(shape, lambda i: (0,) * len(shape))
    return pl.pallas_call(
        _mix_kernel,
        out_shape=(jax.ShapeDtypeStruct((m, D_MODEL), F32),
                   jax.ShapeDtypeStruct((m, LANES), jnp.int32),
                   jax.ShapeDtypeStruct((m, LANES), F32)),
        grid=(m // tm,),
        in_specs=[pl.BlockSpec((tm, SSM_INNER), lambda i: (i, 0)),
                  pl.BlockSpec((tm, ML_INNER), lambda i: (i, 0)),
                  pl.BlockSpec((tm, D_MODEL), lambda i: (i, 10)),
                  pl.BlockSpec((tm, D_MODEL), lambda i: (i, 11)),
                  pl.BlockSpec((tm, D_MODEL), lambda i: (i, 0)),
                  const((SSM_INNER, D_MODEL)), const((ML_INNER, D_MODEL)), const((D_MODEL, D_MODEL)),
                  const((1, D_MODEL)), const((1, D_MODEL)),
                  const((D_MODEL, LANES)), const((1, LANES))],
        out_specs=(pl.BlockSpec((tm, D_MODEL), lambda i: (i, 0)),
                   pl.BlockSpec((tm, LANES), lambda i: (i, 0)),
                   pl.BlockSpec((tm, LANES), lambda i: (i, 0))),
        compiler_params=pltpu.CompilerParams(
            dimension_semantics=("arbitrary",), vmem_limit_bytes=VMEM_LIMIT),
        name="mix",
    )(ya, yb, u_big, u_big, x2d, wpa, wpb, wout, g1, b1, wr, br)


def _moe_kernel(n_prompt, be_ref, nv_ref, used_ref, tok_ref, tokn_ref, dst_ref, rw_ref, hp_ref, hs_ref,
                wgu_ref, bgu_ref, wd_ref, bd_ref, out_ref,
                xb_ref, ob_ref, wgu_bf, wd_bf, gsem, ssem):
    b = pl.program_id(0)
    used = used_ref[0]
    slot = lax.rem(b, 2)
    nslot = 1 - slot

    def gather_copy(tok, dst_slot, r):
        return (pltpu.make_async_copy(hp_ref.at[pl.ds(tok, 1), :], xb_ref.at[dst_slot, pl.ds(r, 1), :],
                                      gsem.at[dst_slot]),
                pltpu.make_async_copy(hs_ref.at[pl.ds(tok - n_prompt, 1), :],
                                      xb_ref.at[dst_slot, pl.ds(r, 1), :], gsem.at[dst_slot]))

    def start_gather(idx_ref, dst_slot):
        def body(r, carry):
            tok = idx_ref[0, 0, r]
            from_p, from_s = gather_copy(tok, dst_slot, r)

            @pl.when(tok < n_prompt)
            def _():
                from_p.start()

            @pl.when(tok >= n_prompt)
            def _():
                from_s.start()
            return carry
        lax.fori_loop(0, MOE_BLOCK, body, 0)

    def wait_gather(dst_slot):
        def body(r, carry):
            pltpu.make_async_copy(hp_ref.at[pl.ds(0, 1), :], xb_ref.at[dst_slot, pl.ds(r, 1), :],
                                  gsem.at[dst_slot]).wait()
            return carry
        lax.fori_loop(0, MOE_BLOCK, body, 0)

    def scatter_copy(src_slot, r, dst):
        return pltpu.make_async_copy(ob_ref.at[src_slot, pl.ds(r, 1), :], out_ref.at[pl.ds(dst, 1), :],
                                     ssem.at[src_slot])

    def start_scatter(src_slot, blk):
        def body(r, carry):
            scatter_copy(src_slot, r, dst_ref[0, 0, r]).start()
            return carry
        lax.fori_loop(0, nv_ref[blk], body, 0)

    def wait_scatter(src_slot, blk):
        def body(r, carry):
            scatter_copy(src_slot, r, 0).wait()
            return carry
        lax.fori_loop(0, nv_ref[blk], body, 0)

    @pl.when(b < used)
    def _():
        @pl.when(b == 0)
        def _():
            start_gather(tok_ref, slot)

        @pl.when(b + 1 < used)
        def _():
            start_gather(tokn_ref, nslot)

        changed = jnp.logical_or(b == 0, be_ref[b] != be_ref[jnp.maximum(b - 1, 0)])

        @pl.when(changed)
        def _():
            wgu_bf[...] = wgu_ref[...].astype(BF16)
            wd_bf[...] = wd_ref[...].astype(BF16)

        wait_gather(slot)

        @pl.when(b >= 2)
        def _():
            wait_scatter(slot, b - 2)

        x = xb_ref[slot].astype(BF16)
        hgu = jnp.dot(x, wgu_bf[...], preferred_element_type=F32) + bgu_ref[...]
        gate = jnp.minimum(hgu[:, :D_FF], SWIGLU_LIMIT)
        up = jnp.clip(hgu[:, D_FF:], -SWIGLU_LIMIT, SWIGLU_LIMIT)
        act = (up + 1.0) * gate * _sigmoid(SWIGLU_ALPHA * gate)
        out = jnp.dot(act.astype(BF16), wd_bf[...], preferred_element_type=F32) + bd_ref[...]
        ob_ref[slot] = out * rw_ref[...]
        start_scatter(slot, b)

        @pl.when(b == used - 1)
        def _():
            @pl.when(b >= 1)
            def _():
                wait_scatter(nslot, b - 1)
            wait_scatter(slot, b)


def _moe(h_p, h_s, block_e, nvalid, used, row_tok, row_dst, row_w, w_gu, b_gu, w_down, b_down, n_rows_out):
    nb = block_e.shape[0]
    n_prompt = h_p.shape[0]
    smem_blk = lambda f: pl.BlockSpec((1, 1, MOE_BLOCK), f, memory_space=pltpu.SMEM)
    grid_spec = pltpu.PrefetchScalarGridSpec(
        num_scalar_prefetch=3,
        grid=(nb,),
        in_specs=[smem_blk(lambda b, be, nv, u: (b, 0, 0)),
                  smem_blk(lambda b, be, nv, u: (jnp.minimum(b + 1, nb - 1), 0, 0)),
                  smem_blk(lambda b, be, nv, u: (b, 0, 0)),
                  pl.BlockSpec((None, MOE_BLOCK, 1), lambda b, be, nv, u: (b, 0, 0)),
                  pl.BlockSpec(memory_space=pl.ANY),
                  pl.BlockSpec(memory_space=pl.ANY),
                  pl.BlockSpec((None, D_MODEL, 2 * D_FF), lambda b, be, nv, u: (be[b], 0, 0)),
                  pl.BlockSpec((None, 1, 2 * D_FF), lambda b, be, nv, u: (be[b], 0, 0)),
                  pl.BlockSpec((None, D_FF, D_MODEL), lambda b, be, nv, u: (be[b], 0, 0)),
                  pl.BlockSpec((None, 1, D_MODEL), lambda b, be, nv, u: (be[b], 0, 0))],
        out_specs=pl.BlockSpec(memory_space=pl.ANY),
        scratch_shapes=[pltpu.VMEM((2, MOE_BLOCK, D_MODEL), F32),
                        pltpu.VMEM((2, MOE_BLOCK, D_MODEL), F32),
                        pltpu.VMEM((D_MODEL, 2 * D_FF), BF16),
                        pltpu.VMEM((D_FF, D_MODEL), BF16),
                        pltpu.SemaphoreType.DMA((2,)),
                        pltpu.SemaphoreType.DMA((2,))])
    return pl.pallas_call(
        functools.partial(_moe_kernel, n_prompt),
        out_shape=jax.ShapeDtypeStruct((n_rows_out, D_MODEL), F32),
        grid_spec=grid_spec,
        compiler_params=pltpu.CompilerParams(
            dimension_semantics=("arbitrary",), vmem_limit_bytes=VMEM_LIMIT),
        name="moe",
    )(block_e, nvalid, used, row_tok, row_tok, row_dst, row_w, h_p, h_s, w_gu, b_gu, w_down, b_down)


def _ln2_kernel(h_ref, e0_ref, e1_ref, e2_ref, e3_ref, g_ref, b_ref, y_ref):
    ffn = (e0_ref[...] + e1_ref[...]) + (e2_ref[...] + e3_ref[...])
    y_ref[...] = _layer_norm(DEEPNORM_ALPHA * h_ref[...] + ffn, g_ref[...], b_ref[...])


def _ln2(h, out2, g2, b2, tok_offset, n_tok_all, tm=512):
    m = h.shape[0]
    per_k = n_tok_all // tm
    off = tok_offset // tm
    espec = lambda k: pl.BlockSpec((tm, D_MODEL), lambda i: (k * per_k + off + i, 0))
    return pl.pallas_call(
        _ln2_kernel,
        out_shape=jax.ShapeDtypeStruct((m, D_MODEL), F32),
        grid=(m // tm,),
        in_specs=[pl.BlockSpec((tm, D_MODEL), lambda i: (i, 0)),
                  espec(0), espec(1), espec(2), espec(3),
                  pl.BlockSpec((1, D_MODEL), lambda i: (0, 0)),
                  pl.BlockSpec((1, D_MODEL), lambda i: (0, 0))],
        out_specs=pl.BlockSpec((tm, D_MODEL), lambda i: (i, 0)),
        compiler_params=pltpu.CompilerParams(dimension_semantics=("arbitrary",)),
        name="ln2",
    )(h, out2, out2, out2, out2, g2, b2)


def _routing_plan(top_i, top_w, n_tok):
    m = n_tok * TOP_K
    nb = m // MOE_BLOCK + N_EXPERTS
    flat_e = top_i.reshape(-1)
    flat_w = top_w.reshape(-1)
    order = jnp.argsort(flat_e).astype(jnp.int32)
    counts = jnp.sum((flat_e[:, None] == jnp.arange(N_EXPERTS)[None, :]).astype(jnp.int32), axis=0)
    nblk = (counts + MOE_BLOCK - 1) // MOE_BLOCK
    bend = jnp.cumsum(nblk)
    bstart = bend - nblk
    cstart = jnp.cumsum(counts) - counts
    used = bend[-1:].astype(jnp.int32)
    blk = jnp.arange(nb, dtype=jnp.int32)
    block_e = jnp.minimum(jnp.searchsorted(bend, blk, side="right"), N_EXPERTS - 1).astype(jnp.int32)
    within = (blk - bstart[block_e]) * MOE_BLOCK
    nvalid = jnp.where(blk < used[0], jnp.clip(counts[block_e] - within, 0, MOE_BLOCK), 0)
    j = jnp.arange(MOE_BLOCK, dtype=jnp.int32)[None, :]
    pos = jnp.minimum(cstart[block_e][:, None] + within[:, None] + j, m - 1)
    valid = j < nvalid[:, None]
    aid = order[pos]
    tok = jnp.where(valid, aid // TOP_K, 0)
    kk = aid % TOP_K
    dst = jnp.where(valid, kk * n_tok + aid // TOP_K, 0)
    roww = jnp.where(valid, flat_w[aid], 0.0)
    return (block_e, nvalid.astype(jnp.int32), used, tok.reshape(nb, 1, MOE_BLOCK).astype(jnp.int32),
            dst.reshape(nb, 1, MOE_BLOCK).astype(jnp.int32), roww.reshape(nb, MOE_BLOCK, 1), m)


def _token_mix(x, states, w, valid_len):
    bsz, seq, _ = x.shape
    x2d = x.reshape(bsz * seq, D_MODEL)
    hist_a, ssm0, hist_b, c0, n0, m0 = states
    tm = min(512, bsz * seq)
    u_big = _inproj(x2d, w["w_big"], w["b_big"], tm=tm).reshape(bsz, seq, U_BIG)
    u_small = _small_proj(x2d, w["w_small"], w["b_small"], tm=tm).reshape(bsz, seq, LANES)

    pad8 = lambda hst: jnp.pad(hst, ((0, 0), (8 - (CONV_K - 1), 0), (0, 0)))
    m0p = jnp.pad(m0, ((0, 0), (0, LANES - ML_H)))[:, None, :]
    ya, ssm = _ssd(u_big, u_small, pad8(hist_a), ssm0.reshape(bsz, SSM_H * SSM_P, SSM_N),
                   w["conv_a_w"], w["conv_a_b"], w["a_log"], w["d_skip_x"], w["norm_a_w"], valid_len)
    yb, c_mem, n_mem, m_mem = _mlstm(u_big, u_small, pad8(hist_b), c0, n0, m0p,
                                     w["conv_b_w"], w["conv_b_b"], w["norm_b_w"], valid_len)
    h, ti, tw = _mix(ya.reshape(bsz * seq, SSM_INNER), yb.reshape(bsz * seq, ML_INNER),
                     u_big.reshape(bsz * seq, U_BIG), x2d, w["w_proj_a"], w["w_proj_b"], w["w_out"],
                     w["ln1_g"], w["ln1_b"], w["w_router"], w["b_router"])
    lo = valid_len - (CONV_K - 1)
    new_states = (u_big[:, seq - CHUNK + lo:seq - CHUNK + valid_len, 0:SSM_XBC],
                  ssm.reshape(bsz, SSM_H, SSM_P, SSM_N),
                  u_big[:, seq - CHUNK + lo:seq - CHUNK + valid_len, 6 * D_MODEL:8 * D_MODEL],
                  c_mem, n_mem, m_mem[:, 0, :ML_H])
    return h, ti[:, :TOP_K], tw[:, :TOP_K], new_states


def _prep_weights(w_in, b_in, conv_a_w, conv_a_b, a_log, d_skip, norm_a_w, conv_b_w, conv_b_b, norm_b_w,
                  w_proj_a, w_proj_b, w_out, ln1_g, ln1_b, w_router, b_router):
    offs = [0]
    for s in IN_SIZES:
        offs.append(offs[-1] + s)
    col = lambda a, i: a[..., offs[i]:offs[i + 1]]
    order_big = (1, 0, 3, 4, 5, 6, 9, 10)
    order_small = (2, 7, 8)
    n_small = SSM_H + 2 * ML_H
    w_small = jnp.concatenate([col(w_in, i) for i in order_small], axis=-1)
    b_small = jnp.concatenate([col(b_in, i) for i in order_small], axis=-1)
    row = lambda a: a.reshape(1, -1)
    return {
        "w_big": jnp.concatenate([col(w_in, i) for i in order_big], axis=-1).astype(BF16),
        "b_big": row(jnp.concatenate([col(b_in, i) for i in order_big], axis=-1)),
        "w_small": jnp.pad(w_small, ((0, 0), (0, LANES - n_small))),
        "b_small": row(jnp.pad(b_small, (0, LANES - n_small))),
        "conv_a_w": conv_a_w, "conv_a_b": row(conv_a_b),
        "a_log": row(jnp.pad(a_log, (0, LANES - SSM_H))),
        "d_skip_x": row(jnp.repeat(d_skip, SSM_P)), "norm_a_w": row(norm_a_w),
        "conv_b_w": conv_b_w, "conv_b_b": row(conv_b_b), "norm_b_w": row(norm_b_w),
        "w_proj_a": w_proj_a.astype(BF16), "w_proj_b": w_proj_b.astype(BF16), "w_out": w_out.astype(BF16),
        "ln1_g": row(ln1_g), "ln1_b": row(ln1_b),
        "w_router": jnp.pad(w_router, ((0, 0), (0, LANES - N_EXPERTS))),
        "b_router": row(jnp.pad(b_router, (0, LANES - N_EXPERTS), constant_values=NEG_BIG)),
    }


def kernel(x_prompt, x_sample, state_conv_a, state_ssm, state_conv_b, state_mlstm_c, state_mlstm_n, state_mlstm_m, w_in, b_in, conv_a_w, conv_a_b, a_log, d_skip, norm_a_w, conv_b_w, conv_b_b, norm_b_w, w_proj_a, w_proj_b, w_out, ln1_g, ln1_b, w_router, b_router, w_gu, b_gu, w_down, b_down, ln2_g, ln2_b):
    assert w_in.shape[0] == 1, "single layer"
    bsz, seq, _ = x_prompt.shape
    dbsz, dseq, _ = x_sample.shape
    assert seq % CHUNK == 0 and CONV_K - 1 <= dseq <= CHUNK
    w = _prep_weights(w_in[0], b_in[0], conv_a_w[0], conv_a_b[0], a_log[0], d_skip[0], norm_a_w[0],
                      conv_b_w[0], conv_b_b[0], norm_b_w[0], w_proj_a[0], w_proj_b[0], w_out[0],
                      ln1_g[0], ln1_b[0], w_router[0], b_router[0])

    zero_states = (jnp.zeros((bsz, CONV_K - 1, SSM_XBC), F32),
                   jnp.zeros((bsz, SSM_H, SSM_P, SSM_N), F32),
                   jnp.zeros((bsz, CONV_K - 1, 2 * ML_INNER), F32),
                   jnp.zeros((bsz, ML_H, ML_D, ML_D), F32),
                   jnp.zeros((bsz, ML_H, ML_D), F32),
                   jnp.zeros((bsz, ML_H), F32))
    past = (state_conv_a[0], state_ssm[0], state_conv_b[0], state_mlstm_c[0], state_mlstm_n[0], state_mlstm_m[0])
    xs_pad = jnp.pad(x_sample, ((0, 0), (0, CHUNK - dseq), (0, 0)))

    h_p, ti_p, tw_p, st_p = _token_mix(x_prompt, zero_states, w, CHUNK)
    h_s, ti_s, tw_s, st_s = _token_mix(xs_pad, past, w, dseq)

    n_p, n_s = h_p.shape[0], h_s.shape[0]
    n_tok = n_p + n_s
    top_i = jnp.concatenate([ti_p, ti_s], axis=0)
    top_w = jnp.concatenate([tw_p, tw_s], axis=0)
    block_e, nvalid, used, row_tok, row_dst, row_w, n_rows_out = _routing_plan(top_i, top_w, n_tok)
    out2 = _moe(h_p, h_s, block_e, nvalid, used, row_tok, row_dst, row_w,
                w_gu[0], b_gu[0][:, None, :], w_down[0], b_down[0][:, None, :], n_rows_out)
    g2, b2 = ln2_g[0].reshape(1, -1), ln2_b[0].reshape(1, -1)
    y_p = _ln2(h_p, out2, g2, b2, 0, n_tok).reshape(bsz, seq, D_MODEL)
    y_s = _ln2(h_s, out2, g2, b2, n_p, n_tok).reshape(dbsz, CHUNK, D_MODEL)[:, :dseq]
    return (y_p, y_s) + tuple(s[None] for s in st_p) + tuple(s[None] for s in st_s)
```

```python
import functools

import jax
import jax.numpy as jnp
from jax import lax
from jax.experimental import pallas as pl
from jax.experimental.pallas import tpu as pltpu

F32 = jnp.float32
BF16 = jnp.bfloat16

D_MODEL = 1024
CHUNK = 64
SSM_INNER = 2 * D_MODEL
SSM_P = 64
SSM_H = SSM_INNER // SSM_P
SSM_G = 8
SSM_J = SSM_H // SSM_G
SSM_N = 128
CONV_K = 4
SSM_XBC = SSM_INNER + 2 * SSM_G * SSM_N
ML_INNER = D_MODEL
ML_H = 4
ML_D = ML_INNER // ML_H
N_EXPERTS = 32
TOP_K = 4
D_FF = D_MODEL
SWIGLU_LIMIT = 7.0
SWIGLU_ALPHA = 1.702
MOE_ROWS = 256
MOE_COLS = 256
DEEPNORM_ALPHA = 2.0 ** 0.25
NORM_EPS = 1e-5
IN_SIZES = (SSM_INNER, SSM_XBC, SSM_H, ML_INNER, ML_INNER, ML_INNER, ML_INNER, ML_H, ML_H, D_MODEL, D_MODEL)

LANES = 128
GW = SSM_J * SSM_P
U_BIG = 12 * D_MODEL
IG_LANE = SSM_H
FG_LANE = SSM_H + ML_H
NEG_BIG = -1e30
VMEM_LIMIT = 56 * 1024 * 1024


def _sigmoid(x):
    return 1.0 / (1.0 + jnp.exp(-x))


def _softplus(x):
    return jnp.maximum(x, 0.0) + jnp.log(1.0 + jnp.exp(-jnp.abs(x)))


def _prefix_sum_rows(x):
    n = x.shape[0]
    row = lax.broadcasted_iota(jnp.int32, x.shape, 0)
    k = 1
    while k < n:
        x = x + jnp.where(row >= k, pltpu.roll(x, k, 0), 0.0)
        k *= 2
    return x


def _split3(x):
    hi = x.astype(BF16)
    r1 = x - hi.astype(F32)
    mid = r1.astype(BF16)
    lo = (r1 - mid.astype(F32)).astype(BF16)
    return hi, mid, lo


def _inproj_kernel(x_ref, w_ref, b_ref, o_ref):
    x = x_ref[...].astype(BF16)
    o_ref[...] = jnp.dot(x, w_ref[...], preferred_element_type=F32) + b_ref[...]


def _inproj(x2d, w_big, b_big, tm=512, tn=2048):
    m = x2d.shape[0]
    return pl.pallas_call(
        _inproj_kernel,
        out_shape=jax.ShapeDtypeStruct((m, U_BIG), F32),
        grid=(U_BIG // tn, m // tm),
        in_specs=[pl.BlockSpec((tm, D_MODEL), lambda j, i: (i, 0)),
                  pl.BlockSpec((D_MODEL, tn), lambda j, i: (0, j)),
                  pl.BlockSpec((1, tn), lambda j, i: (0, j))],
        out_specs=pl.BlockSpec((tm, tn), lambda j, i: (i, j)),
        compiler_params=pltpu.CompilerParams(
            dimension_semantics=("arbitrary", "arbitrary"), vmem_limit_bytes=VMEM_LIMIT),
        name="inproj",
    )(x2d, w_big, b_big)


def _small_kernel(x_ref, w_ref, b_ref, o_ref):
    o_ref[...] = jnp.dot(x_ref[...], w_ref[...], preferred_element_type=F32,
                         precision=lax.Precision.HIGHEST) + b_ref[...]


def _small_proj(x2d, w_small, b_small, tm=512):
    m = x2d.shape[0]
    return pl.pallas_call(
        _small_kernel,
        out_shape=jax.ShapeDtypeStruct((m, LANES), F32),
        grid=(m // tm,),
        in_specs=[pl.BlockSpec((tm, D_MODEL), lambda i: (i, 0)),
                  pl.BlockSpec((D_MODEL, LANES), lambda i: (0, 0)),
                  pl.BlockSpec((1, LANES), lambda i: (0, 0))],
        out_specs=pl.BlockSpec((tm, LANES), lambda i: (i, 0)),
        compiler_params=pltpu.CompilerParams(dimension_semantics=("arbitrary",)),
        name="small_proj",
    )(x2d, w_small, b_small)


def _conv_chunk(u_ref, hist_ref, cw_ref, cb_ref, ext_ref, first):
    @pl.when(first)
    def _():
        ext_ref[0:8, :] = hist_ref[...]

    ext_ref[8:8 + CHUNK, :] = u_ref[...]
    acc = cb_ref[...]
    for j in range(CONV_K):
        lo = 8 - (CONV_K - 1) + j
        acc = acc + ext_ref[lo:lo + CHUNK, :] * cw_ref[j:j + 1, :]
    ext_ref[0:8, :] = ext_ref[CHUNK:CHUNK + 8, :]
    return acc


def _ssd_kernel(valid_len, xbc_ref, z_ref, sm_ref, hist_ref, s0_ref, cw_ref, cb_ref, alog_ref,
                dskip_ref, nw_ref, exp_ref, eye_ref, causal_ref, bd_ref,
                ya_ref, sout_ref, ext_ref, st_ref):
    c = pl.program_id(1)
    nc = pl.num_programs(1)

    @pl.when(c == 0)
    def _():
        st_ref[...] = s0_ref[...].T

    act = _conv_chunk(xbc_ref, hist_ref, cw_ref, cb_ref, ext_ref, c == 0)
    act = act * _sigmoid(act)
    xs = act[:, :SSM_INNER]
    bm_f = act[:, SSM_INNER:SSM_INNER + SSM_G * SSM_N]
    bm = bm_f.astype(BF16)
    cm = act[:, SSM_INNER + SSM_G * SSM_N:].astype(BF16)

    dt = _softplus(sm_ref[...])
    if valid_len < CHUNK:
        row = lax.broadcasted_iota(jnp.int32, dt.shape, 0)
        dt = jnp.where(row < valid_len, dt, 0.0)
    a = -jnp.exp(alog_ref[...])
    cum = _prefix_sum_rows(dt * a)

    pieces = _split3(cum) + _split3(dt)
    stacked = jnp.concatenate(pieces, axis=0)
    r = jnp.dot(stacked, exp_ref[...], preferred_element_type=F32)
    cum_x = r[0:CHUNK] + r[CHUNK:2 * CHUNK] + r[2 * CHUNK:3 * CHUNK]
    dt_x = r[3 * CHUNK:4 * CHUNK] + r[4 * CHUNK:5 * CHUNK] + r[5 * CHUNK:6 * CHUNK]

    eye = eye_ref[...]
    cum_row = jnp.sum(cum_x * eye, axis=0, keepdims=True)
    dt_row = jnp.sum(dt_x * eye, axis=0, keepdims=True)
    cum_last = cum_x[CHUNK - 1:CHUNK, :]

    decay = jnp.where(causal_ref[...] > 0.0, jnp.exp(cum_x - cum_row), 0.0)
    expc_x = jnp.exp(cum_x)
    tail_x = dt_x * jnp.exp(cum_last - cum_x)
    dec_x = jnp.exp(cum_last)

    xt = (xs * tail_x).astype(BF16)
    xs_b = xs.astype(BF16)
    bdmask = bd_ref[...] > 0.0
    zrow_b = jnp.zeros((CHUNK, SSM_N), F32)
    zrow_x = jnp.zeros((CHUNK, GW), BF16)
    ys = []
    for g in range(SSM_G):
        gs = slice(g * GW, (g + 1) * GW)
        ns = slice(g * SSM_N, (g + 1) * SSM_N)
        cg = cm[:, ns]
        bg = bm[:, ns]
        btile = jnp.concatenate([bg] * SSM_J, axis=0)
        cb = lax.dot_general(cg, btile, (((1,), (1,)), ((), ())), preferred_element_type=F32)
        w = (decay[:, gs] * cb * dt_row[:, gs]).astype(BF16)
        xg = xs_b[:, gs]
        bd = jnp.where(bdmask, jnp.concatenate([xg] * SSM_J, axis=0), jnp.zeros((), BF16))
        y = jnp.dot(w, bd, preferred_element_type=F32)
        s_g = st_ref[:, gs]
        y = y + jnp.dot(cg, s_g.astype(BF16), preferred_element_type=F32) * expc_x[:, gs]
        ys.append(y)
        bgt = jnp.concatenate([bm_f[:, ns], zrow_b], axis=0).T.astype(BF16)
        xtg = jnp.concatenate([xt[:, gs], zrow_x], axis=0)
        st_ref[:, gs] = dec_x[:, gs] * s_g + jnp.dot(bgt, xtg, preferred_element_type=F32)

    y = jnp.concatenate(ys, axis=1) + dskip_ref[...] * xs
    zz = z_ref[...]
    y = y * (zz * _sigmoid(zz))
    outs = []
    for g in range(SSM_G):
        yg = y[:, g * GW:(g + 1) * GW]
        outs.append(yg * lax.rsqrt(jnp.mean(yg * yg, axis=-1, keepdims=True) + NORM_EPS))
    ya_ref[...] = (jnp.concatenate(outs, axis=1) * nw_ref[...]).astype(ya_ref.dtype)

    @pl.when(c == nc - 1)
    def _():
        sout_ref[...] = st_ref[...].T


def _ssd_consts():
    lane = jnp.arange(SSM_INNER)
    expand = (jnp.arange(LANES)[:, None] == (lane // SSM_P)[None, :]).astype(BF16)
    t = jnp.arange(CHUNK)[:, None]
    s = (lane % SSM_P)[None, :]
    eye = (t == s).astype(F32)
    causal = (s <= t).astype(F32)
    r = jnp.arange(GW)
    bd = ((r[:, None] // SSM_P) == (r[None, :] // SSM_P)).astype(F32)
    return expand, eye, causal, bd


def _ssd(u_big, u_small, hist8, s0, cw, cb, alog, dskip_x, nw, valid_len):
    bsz, seq, _ = u_big.shape
    nc = seq // CHUNK
    expand, eye, causal, bd = _ssd_consts()
    const = lambda shape: pl.BlockSpec(shape, lambda b, c: (0,) * len(shape))
    return pl.pallas_call(
        functools.partial(_ssd_kernel, valid_len),
        out_shape=(jax.ShapeDtypeStruct((bsz, seq, SSM_INNER), BF16),
                   jax.ShapeDtypeStruct((bsz, SSM_H * SSM_P, SSM_N), F32)),
        grid=(bsz, nc),
        in_specs=[pl.BlockSpec((None, CHUNK, SSM_XBC), lambda b, c: (b, c, 0)),
                  pl.BlockSpec((None, CHUNK, SSM_INNER), lambda b, c: (b, c, 2)),
                  pl.BlockSpec((None, CHUNK, LANES), lambda b, c: (b, c, 0)),
                  pl.BlockSpec((None, 8, SSM_XBC), lambda b, c: (b, 0, 0)),
                  pl.BlockSpec((None, SSM_H * SSM_P, SSM_N), lambda b, c: (b, 0, 0)),
                  const((CONV_K, SSM_XBC)), const((1, SSM_XBC)), const((1, LANES)),
                  const((1, SSM_INNER)), const((1, SSM_INNER)),
                  const((LANES, SSM_INNER)), const((CHUNK, SSM_INNER)), const((CHUNK, SSM_INNER)),
                  const((GW, GW))],
        out_specs=(pl.BlockSpec((None, CHUNK, SSM_INNER), lambda b, c: (b, c, 0)),
                   pl.BlockSpec((None, SSM_H * SSM_P, SSM_N), lambda b, c: (b, 0, 0))),
        scratch_shapes=[pltpu.VMEM((CHUNK + 8, SSM_XBC), F32),
                        pltpu.VMEM((SSM_N, SSM_H * SSM_P), F32)],
        compiler_params=pltpu.CompilerParams(
            dimension_semantics=("arbitrary", "arbitrary"), vmem_limit_bytes=VMEM_LIMIT),
        name="ssd",
    )(u_big, u_big, u_small, hist8, s0, cw, cb, alog, dskip_x, nw, expand, eye, causal, bd)


def _mlstm_kernel(valid_len, qk_ref, v_ref, o_ref, sm_ref, hist_ref, c0_ref, n0_ref, m0_ref,
                  cw_ref, cb_ref, nw_ref,
                  yb_ref, cout_ref, nout_ref, mout_ref, ext_ref, c_ref, n_ref, m_ref):
    c = pl.program_id(1)
    nc = pl.num_programs(1)

    @pl.when(c == 0)
    def _():
        c_ref[...] = c0_ref[...]
        n_ref[...] = n0_ref[...]
        m_ref[...] = m0_ref[...]

    act = _conv_chunk(qk_ref, hist_ref, cw_ref, cb_ref, ext_ref, c == 0)
    act = act * _sigmoid(act)
    vv = v_ref[...]

    sm = sm_ref[...]
    lf = -_softplus(-sm)
    row = lax.broadcasted_iota(jnp.int32, sm.shape, 0)
    lane = lax.broadcasted_iota(jnp.int32, sm.shape, 1)
    if valid_len < CHUNK:
        lf = jnp.where(row < valid_len, lf, 0.0)
        sm = jnp.where(row < valid_len, sm, NEG_BIG)
    bcum = _prefix_sum_rows(lf)
    is_f = (lane >= FG_LANE) & (lane < FG_LANE + ML_H)
    both = jnp.where(is_f, bcum, sm)
    both_t = jnp.concatenate([both, jnp.zeros_like(both)], axis=0).T

    tri = lax.broadcasted_iota(jnp.int32, (CHUNK, CHUNK), 1) <= lax.broadcasted_iota(jnp.int32, (CHUNK, CHUNK), 0)
    lane_row = lax.broadcasted_iota(jnp.int32, (1, LANES), 1)
    m_all = m_ref[...]
    m_next = m_all
    zrow = jnp.zeros((CHUNK, ML_D), BF16)
    zrow_f = jnp.zeros((CHUNK, ML_D), F32)
    hs = []
    for h in range(ML_H):
        ds = slice(h * ML_D, (h + 1) * ML_D)
        q = act[:, ds]
        k = act[:, ML_INNER + h * ML_D:ML_INNER + (h + 1) * ML_D] * (ML_D ** -0.5)
        v = vv[:, ds]
        qb, kb = q.astype(BF16), k.astype(BF16)
        bc = bcum[:, FG_LANE + h:FG_LANE + h + 1]
        igc = sm[:, IG_LANE + h:IG_LANE + h + 1]
        bct = both_t[FG_LANE + h:FG_LANE + h + 1, 0:CHUNK]
        igt = both_t[IG_LANE + h:IG_LANE + h + 1, 0:CHUNK]
        m_prev = m_all[:, h:h + 1]
        dmat = jnp.where(tri, bc - bct + igt, -jnp.inf)
        inter = bc + m_prev
        m_t = jnp.maximum(inter, jnp.max(dmat, axis=1, keepdims=True))
        wts = jnp.exp(dmat - m_t)
        w_in = jnp.exp(inter - m_t)
        s = lax.dot_general(qb, kb, (((1,), (1,)), ((), ())), preferred_element_type=F32) * wts
        c_prev = c_ref[h]
        n_prev = n_ref[h:h + 1, :]
        qc = lax.dot_general(qb, c_prev.astype(BF16), (((1,), (1,)), ((), ())), preferred_element_type=F32)
        num = jnp.dot(s.astype(BF16), v.astype(BF16), preferred_element_type=F32) + w_in * qc
        den = jnp.sum(s, axis=1, keepdims=True) + w_in * jnp.sum(q * n_prev, axis=1, keepdims=True)
        hh = num / jnp.maximum(jnp.abs(den), jnp.exp(-m_t))
        m_new = m_t[CHUNK - 1:CHUNK, :]
        bc_last = bc[CHUNK - 1:CHUNK, :]
        w_tail = jnp.exp(bc_last - bc + igc - m_new)
        keep = jnp.exp(bc_last + m_prev - m_new)
        vw_t = jnp.concatenate([v * w_tail, zrow_f], axis=0).T.astype(BF16)
        kpad = jnp.concatenate([kb, zrow], axis=0)
        c_ref[h] = keep * c_prev + jnp.dot(vw_t, kpad, preferred_element_type=F32)
        n_ref[h:h + 1, :] = keep * n_prev + jnp.sum(k * w_tail, axis=0, keepdims=True)
        m_next = jnp.where(lane_row == h, m_new, m_next)
        hs.append(hh * lax.rsqrt(jnp.mean(hh * hh, axis=-1, keepdims=True) + NORM_EPS))
    m_ref[...] = m_next
    hn = jnp.concatenate(hs, axis=1) * nw_ref[...]
    yb_ref[...] = (_sigmoid(o_ref[...]) * hn).astype(yb_ref.dtype)

    @pl.when(c == nc - 1)
    def _():
        cout_ref[...] = c_ref[...]
        nout_ref[...] = n_ref[...]
        mout_ref[...] = m_ref[...]


def _mlstm(u_big, u_small, hist8, c0, n0, m0p, cw, cb, nw, valid_len):
    bsz, seq, _ = u_big.shape
    nc = seq // CHUNK
    const = lambda shape: pl.BlockSpec(shape, lambda b, c: (0,) * len(shape))
    return pl.pallas_call(
        functools.partial(_mlstm_kernel, valid_len),
        out_shape=(jax.ShapeDtypeStruct((bsz, seq, ML_INNER), BF16),
                   jax.ShapeDtypeStruct((bsz, ML_H, ML_D, ML_D), F32),
                   jax.ShapeDtypeStruct((bsz, ML_H, ML_D), F32),
                   jax.ShapeDtypeStruct((bsz, 1, LANES), F32)),
        grid=(bsz, nc),
        in_specs=[pl.BlockSpec((None, CHUNK, 2 * ML_INNER), lambda b, c: (b, c, 3)),
                  pl.BlockSpec((None, CHUNK, ML_INNER), lambda b, c: (b, c, 8)),
                  pl.BlockSpec((None, CHUNK, ML_INNER), lambda b, c: (b, c, 9)),
                  pl.BlockSpec((None, CHUNK, LANES), lambda b, c: (b, c, 0)),
                  pl.BlockSpec((None, 8, 2 * ML_INNER), lambda b, c: (b, 0, 0)),
                  pl.BlockSpec((None, ML_H, ML_D, ML_D), lambda b, c: (b, 0, 0, 0)),
                  pl.BlockSpec((None, ML_H, ML_D), lambda b, c: (b, 0, 0)),
                  pl.BlockSpec((None, 1, LANES), lambda b, c: (b, 0, 0)),
                  const((CONV_K, 2 * ML_INNER)), const((1, 2 * ML_INNER)), const((1, ML_INNER))],
        out_specs=(pl.BlockSpec((None, CHUNK, ML_INNER), lambda b, c: (b, c, 0)),
                   pl.BlockSpec((None, ML_H, ML_D, ML_D), lambda b, c: (b, 0, 0, 0)),
                   pl.BlockSpec((None, ML_H, ML_D), lambda b, c: (b, 0, 0)),
                   pl.BlockSpec((None, 1, LANES), lambda b, c: (b, 0, 0))),
        scratch_shapes=[pltpu.VMEM((CHUNK + 8, 2 * ML_INNER), F32),
                        pltpu.VMEM((ML_H, ML_D, ML_D), F32),
                        pltpu.VMEM((ML_H, ML_D), F32),
                        pltpu.VMEM((1, LANES), F32)],
        compiler_params=pltpu.CompilerParams(
            dimension_semantics=("arbitrary", "arbitrary"), vmem_limit_bytes=VMEM_LIMIT),
        name="mlstm",
    )(u_big, u_big, u_big, u_small, hist8, c0, n0, m0p, cw, cb, nw)


def _layer_norm(r, g, b):
    mu = jnp.mean(r, axis=-1, keepdims=True)
    rc = r - mu
    var = jnp.mean(rc * rc, axis=-1, keepdims=True)
    return rc * lax.rsqrt(var + NORM_EPS) * g + b


def _mix_kernel(ya_ref, yb_ref, ga_ref, gb_ref, x_ref, wpa_ref, wpb_ref, wout_ref, g_ref, b_ref,
                wr_ref, br_ref, h_ref, ti_ref, tw_ref):
    pa = jnp.dot(ya_ref[...], wpa_ref[...], preferred_element_type=F32)
    pb = jnp.dot(yb_ref[...], wpb_ref[...], preferred_element_type=F32)
    merged = _sigmoid(ga_ref[...]) * pa + _sigmoid(gb_ref[...]) * pb
    out = jnp.dot(merged.astype(BF16), wout_ref[...], preferred_element_type=F32)
    h = _layer_norm(DEEPNORM_ALPHA * x_ref[...] + out, g_ref[...], b_ref[...])
    h_ref[...] = h

    logits = jnp.dot(h, wr_ref[...], preferred_element_type=F32,
                     precision=lax.Precision.HIGHEST) + br_ref[...]
    lane = lax.broadcasted_iota(jnp.int32, logits.shape, 1)
    lane_f = lane.astype(F32)
    vals, idxs = [], []
    cur = logits
    for _ in range(TOP_K):
        m = jnp.max(cur, axis=1, keepdims=True)
        idx = jnp.min(jnp.where(cur == m, lane_f, float(LANES)), axis=1, keepdims=True)
        vals.append(m)
        idxs.append(idx)
        cur = jnp.where(lane_f == idx, -jnp.inf, cur)
    es = [jnp.exp(v - vals[0]) for v in vals]
    tot = es[0] + es[1] + es[2] + es[3]
    ti = jnp.zeros(logits.shape, F32)
    tw = jnp.zeros(logits.shape, F32)
    for k in range(TOP_K):
        ti = jnp.where(lane == k, idxs[k], ti)
        tw = jnp.where(lane == k, es[k] / tot, tw)
    ti_ref[...] = ti.astype(jnp.int32)
    tw_ref[...] = tw


def _mix(ya, yb, u_big, x2d, wpa, wpb, wout, g1, b1, wr, br, tm=256):
    m = x2d.shape[0]
    tm = min(tm, m)
    const = lambda shape: pl.BlockSpec(shape, lambda i: (0,) * len(shape))
    return pl.pallas_call(
        _mix_kernel,
        out_shape=(jax.ShapeDtypeStruct((m, D_MODEL), F32),
                   jax.ShapeDtypeStruct((m, LANES), jnp.int32),
                   jax.ShapeDtypeStruct((m, LANES), F32)),
        grid=(m // tm,),
        in_specs=[pl.BlockSpec((tm, SSM_INNER), lambda i: (i, 0)),
                  pl.BlockSpec((tm, ML_INNER), lambda i: (i, 0)),
                  pl.BlockSpec((tm, D_MODEL), lambda i: (i, 10)),
                  pl.BlockSpec((tm, D_MODEL), lambda i: (i, 11)),
                  pl.BlockSpec((tm, D_MODEL), lambda i: (i, 0)),
                  const((SSM_INNER, D_MODEL)), const((ML_INNER, D_MODEL)), const((D_MODEL, D_MODEL)),
                  const((1, D_MODEL)), const((1, D_MODEL)),
                  const((D_MODEL, LANES)), const((1, LANES))],
        out_specs=(pl.BlockSpec((tm, D_MODEL), lambda i: (i, 0)),
                   pl.BlockSpec((tm, LANES), lambda i: (i, 0)),
                   pl.BlockSpec((tm, LANES), lambda i: (i, 0))),
        compiler_params=pltpu.CompilerParams(
            dimension_semantics=("arbitrary",), vmem_limit_bytes=VMEM_LIMIT),
        name="mix",
    )(ya, yb, u_big, u_big, x2d, wpa, wpb, wout, g1, b1, wr, br)


def _moe_kernel(be_ref, used_ref, tok_ref, tokn_ref, dstp_ref, dst_ref, rw_ref, h_ref,
                wgu_ref, bgu_ref, wd_ref, bd_ref, out_ref,
                xb_ref, ob_ref, wgu_bf, wd_bf, gsem, ssem):
    b = pl.program_id(0)
    used = used_ref[0]
    slot = lax.rem(b, 2)
    nslot = 1 - slot

    def gather(src_row, dst_slot, r):
        return pltpu.make_async_copy(h_ref.at[pl.ds(src_row, 1), :], xb_ref.at[dst_slot, pl.ds(r, 1), :],
                                     gsem.at[dst_slot])

    def scatter(src_slot, r, dst_row):
        return pltpu.make_async_copy(ob_ref.at[src_slot, pl.ds(r, 1), :], out_ref.at[pl.ds(dst_row, 1), :],
                                     ssem.at[src_slot])

    def for_rows(fn):
        def body(r, carry):
            fn(r)
            return carry
        lax.fori_loop(0, MOE_ROWS, body, 0)

    @pl.when(b < used)
    def _():
        @pl.when(b == 0)
        def _():
            ob_ref[1] = jnp.zeros((MOE_ROWS, D_MODEL), F32)
            for_rows(lambda r: gather(tok_ref[0, 0, r], 0, r).start())

        changed = jnp.logical_or(b == 0, be_ref[b] != be_ref[jnp.maximum(b - 1, 0)])

        @pl.when(changed)
        def _():
            wgu_bf[...] = wgu_ref[...].astype(BF16)
            wd_bf[...] = wd_ref[...].astype(BF16)

        for r in range(MOE_ROWS):
            gather(0, slot, r).wait()

        @pl.when(b >= 1)
        def _():
            for r in range(MOE_ROWS):
                scatter(slot, r, 0).wait()

        x = xb_ref[slot].astype(BF16)
        per = MOE_ROWS // (D_FF // MOE_COLS)
        acts = []
        for c in range(D_FF // MOE_COLS):
            for r in range(c * per, (c + 1) * per):
                gather(tokn_ref[0, 0, r], nslot, r).start()
                scatter(nslot, r, dstp_ref[0, 0, r]).start(priority=1)
            gs = slice(c * MOE_COLS, (c + 1) * MOE_COLS)
            us = slice(D_FF + c * MOE_COLS, D_FF + (c + 1) * MOE_COLS)
            g = jnp.dot(x, wgu_bf[:, gs], preferred_element_type=F32) + bgu_ref[:, gs]
            u = jnp.dot(x, wgu_bf[:, us], preferred_element_type=F32) + bgu_ref[:, us]
            gate = jnp.minimum(g, SWIGLU_LIMIT)
            up = jnp.clip(u, -SWIGLU_LIMIT, SWIGLU_LIMIT)
            acts.append(((up + 1.0) * gate * _sigmoid(SWIGLU_ALPHA * gate)).astype(BF16))
        act = jnp.concatenate(acts, axis=1)
        out = jnp.dot(act, wd_bf[...], preferred_element_type=F32) + bd_ref[...]
        ob_ref[slot] = out * rw_ref[...]

        @pl.when(b == used - 1)
        def _():
            for_rows(lambda r: scatter(slot, r, dst_ref[0, 0, r]).start())
            for_rows(lambda r: scatter(nslot, r, 0).wait())
            for_rows(lambda r: scatter(slot, r, 0).wait())
            for_rows(lambda r: gather(0, nslot, r).wait())


def _moe(h_all, block_e, used, row_tok, dst_prev, row_dst, row_w, w_gu, b_gu, w_down, b_down, n_rows_out):
    nb = block_e.shape[0]
    smem_blk = lambda f: pl.BlockSpec((1, 1, MOE_ROWS), f, memory_space=pltpu.SMEM)
    grid_spec = pltpu.PrefetchScalarGridSpec(
        num_scalar_prefetch=2,
        grid=(nb,),
        in_specs=[smem_blk(lambda b, be, u: (b, 0, 0)),
                  smem_blk(lambda b, be, u: (jnp.minimum(b + 1, nb - 1), 0, 0)),
                  smem_blk(lambda b, be, u: (b, 0, 0)),
                  smem_blk(lambda b, be, u: (b, 0, 0)),
                  pl.BlockSpec((None, MOE_ROWS, 1), lambda b, be, u: (b, 0, 0)),
                  pl.BlockSpec(memory_space=pl.ANY),
                  pl.BlockSpec((None, D_MODEL, 2 * D_FF), lambda b, be, u: (be[b], 0, 0)),
                  pl.BlockSpec((None, 1, 2 * D_FF), lambda b, be, u: (be[b], 0, 0)),
                  pl.BlockSpec((None, D_FF, D_MODEL), lambda b, be, u: (be[b], 0, 0)),
                  pl.BlockSpec((None, 1, D_MODEL), lambda b, be, u: (be[b], 0, 0))],
        out_specs=pl.BlockSpec(memory_space=pl.ANY),
        scratch_shapes=[pltpu.VMEM((2, MOE_ROWS, D_MODEL), F32),
                        pltpu.VMEM((2, MOE_ROWS, D_MODEL), F32),
                        pltpu.VMEM((D_MODEL, 2 * D_FF), BF16),
                        pltpu.VMEM((D_FF, D_MODEL), BF16),
                        pltpu.SemaphoreType.DMA((2,)),
                        pltpu.SemaphoreType.DMA((2,))])
    return pl.pallas_call(
        _moe_kernel,
        out_shape=jax.ShapeDtypeStruct((n_rows_out, D_MODEL), F32),
        grid_spec=grid_spec,
        compiler_params=pltpu.CompilerParams(
            dimension_semantics=("arbitrary",), vmem_limit_bytes=VMEM_LIMIT),
        name="moe",
    )(block_e, used, row_tok, row_tok, dst_prev, row_dst, row_w, h_all, w_gu, b_gu, w_down, b_down)


def _ln2_kernel(h_ref, e0_ref, e1_ref, e2_ref, e3_ref, g_ref, b_ref, y_ref):
    ffn = (e0_ref[...] + e1_ref[...]) + (e2_ref[...] + e3_ref[...])
    y_ref[...] = _layer_norm(DEEPNORM_ALPHA * h_ref[...] + ffn, g_ref[...], b_ref[...])


def _ln2(h, out2, g2, b2, tok_offset, n_tok_all, tm=512):
    m = h.shape[0]
    per_k = n_tok_all // tm
    off = tok_offset // tm
    espec = lambda k: pl.BlockSpec((tm, D_MODEL), lambda i: (k * per_k + off + i, 0))
    return pl.pallas_call(
        _ln2_kernel,
        out_shape=jax.ShapeDtypeStruct((m, D_MODEL), F32),
        grid=(m // tm,),
        in_specs=[pl.BlockSpec((tm, D_MODEL), lambda i: (i, 0)),
                  espec(0), espec(1), espec(2), espec(3),
                  pl.BlockSpec((1, D_MODEL), lambda i: (0, 0)),
                  pl.BlockSpec((1, D_MODEL), lambda i: (0, 0))],
        out_specs=pl.BlockSpec((tm, D_MODEL), lambda i: (i, 0)),
        compiler_params=pltpu.CompilerParams(dimension_semantics=("arbitrary",)),
        name="ln2",
    )(h, out2, out2, out2, out2, g2, b2)


def _routing_plan(top_i, top_w, n_tok):
    m = n_tok * TOP_K
    nb = m // MOE_ROWS + N_EXPERTS
    flat_e = top_i.reshape(-1)
    flat_w = top_w.reshape(-1)
    order = jnp.argsort(flat_e).astype(jnp.int32)
    counts = jnp.sum((flat_e[:, None] == jnp.arange(N_EXPERTS)[None, :]).astype(jnp.int32), axis=0)
    nblk = (counts + MOE_ROWS - 1) // MOE_ROWS
    bend = jnp.cumsum(nblk)
    bstart = bend - nblk
    cstart = jnp.cumsum(counts) - counts
    used = bend[-1:].astype(jnp.int32)
    blk = jnp.arange(nb, dtype=jnp.int32)
    block_e = jnp.minimum(jnp.sum((blk[:, None] >= bend[None, :]).astype(jnp.int32), axis=1), N_EXPERTS - 1)
    within = (blk - bstart[block_e]) * MOE_ROWS
    nvalid = jnp.where(blk < used[0], jnp.clip(counts[block_e] - within, 0, MOE_ROWS), 0)
    j = jnp.arange(MOE_ROWS, dtype=jnp.int32)[None, :]
    pos = jnp.minimum(cstart[block_e][:, None] + within[:, None] + j, m - 1)
    valid = j < nvalid[:, None]
    aid = order[pos]
    tok = jnp.where(valid, aid // TOP_K, 0)
    spare = jnp.broadcast_to(m + j, (1, MOE_ROWS))
    dst = jnp.where(valid, (aid % TOP_K) * n_tok + aid // TOP_K, spare).astype(jnp.int32)
    dst_prev = jnp.concatenate([spare, dst[:-1]], axis=0)
    roww = jnp.where(valid, flat_w[aid], 0.0)
    shp = (nb, 1, MOE_ROWS)
    return (block_e.astype(jnp.int32), used, tok.reshape(shp).astype(jnp.int32), dst_prev.reshape(shp),
            dst.reshape(shp), roww.reshape(nb, MOE_ROWS, 1), m + MOE_ROWS)


def _token_mix(x, states, w, valid_len):
    bsz, seq, _ = x.shape
    x2d = x.reshape(bsz * seq, D_MODEL)
    hist_a, ssm0, hist_b, c0, n0, m0 = states
    tm = min(512, bsz * seq)
    u_big = _inproj(x2d, w["w_big"], w["b_big"], tm=tm).reshape(bsz, seq, U_BIG)
    u_small = _small_proj(x2d, w["w_small"], w["b_small"], tm=tm).reshape(bsz, seq, LANES)

    pad8 = lambda hst: jnp.pad(hst, ((0, 0), (8 - (CONV_K - 1), 0), (0, 0)))
    m0p = jnp.pad(m0, ((0, 0), (0, LANES - ML_H)))[:, None, :]
    ya, ssm = _ssd(u_big, u_small, pad8(hist_a), ssm0.reshape(bsz, SSM_H * SSM_P, SSM_N),
                   w["conv_a_w"], w["conv_a_b"], w["a_log"], w["d_skip_x"], w["norm_a_w"], valid_len)
    yb, c_mem, n_mem, m_mem = _mlstm(u_big, u_small, pad8(hist_b), c0, n0, m0p,
                                     w["conv_b_w"], w["conv_b_b"], w["norm_b_w"], valid_len)
    h, ti, tw = _mix(ya.reshape(bsz * seq, SSM_INNER), yb.reshape(bsz * seq, ML_INNER),
                     u_big.reshape(bsz * seq, U_BIG), x2d, w["w_proj_a"], w["w_proj_b"], w["w_out"],
                     w["ln1_g"], w["ln1_b"], w["w_router"], w["b_router"])
    lo = valid_len - (CONV_K - 1)
    new_states = (u_big[:, seq - CHUNK + lo:seq - CHUNK + valid_len, 0:SSM_XBC],
                  ssm.reshape(bsz, SSM_H, SSM_P, SSM_N),
                  u_big[:, seq - CHUNK + lo:seq - CHUNK + valid_len, 6 * D_MODEL:8 * D_MODEL],
                  c_mem, n_mem, m_mem[:, 0, :ML_H])
    return h, ti[:, :TOP_K], tw[:, :TOP_K], new_states


def _prep_weights(w_in, b_in, conv_a_w, conv_a_b, a_log, d_skip, norm_a_w, conv_b_w, conv_b_b, norm_b_w,
                  w_proj_a, w_proj_b, w_out, ln1_g, ln1_b, w_router, b_router):
    offs = [0]
    for s in IN_SIZES:
        offs.append(offs[-1] + s)
    col = lambda a, i: a[..., offs[i]:offs[i + 1]]
    order_big = (1, 0, 3, 4, 5, 6, 9, 10)
    order_small = (2, 7, 8)
    n_small = SSM_H + 2 * ML_H
    w_small = jnp.concatenate([col(w_in, i) for i in order_small], axis=-1)
    b_small = jnp.concatenate([col(b_in, i) for i in order_small], axis=-1)
    row = lambda a: a.reshape(1, -1)
    return {
        "w_big": jnp.concatenate([col(w_in, i) for i in order_big], axis=-1).astype(BF16),
        "b_big": row(jnp.concatenate([col(b_in, i) for i in order_big], axis=-1)),
        "w_small": jnp.pad(w_small, ((0, 0), (0, LANES - n_small))),
        "b_small": row(jnp.pad(b_small, (0, LANES - n_small))),
        "conv_a_w": conv_a_w, "conv_a_b": row(conv_a_b),
        "a_log": row(jnp.pad(a_log, (0, LANES - SSM_H))),
        "d_skip_x": row(jnp.repeat(d_skip, SSM_P)), "norm_a_w": row(norm_a_w),
        "conv_b_w": conv_b_w, "conv_b_b": row(conv_b_b), "norm_b_w": row(norm_b_w),
        "w_proj_a": w_proj_a.astype(BF16), "w_proj_b": w_proj_b.astype(BF16), "w_out": w_out.astype(BF16),
        "ln1_g": row(ln1_g), "ln1_b": row(ln1_b),
        "w_router": jnp.pad(w_router, ((0, 0), (0, LANES - N_EXPERTS))),
        "b_router": row(jnp.pad(b_router, (0, LANES - N_EXPERTS), constant_values=NEG_BIG)),
    }


def kernel(x_prompt, x_sample, state_conv_a, state_ssm, state_conv_b, state_mlstm_c, state_mlstm_n, state_mlstm_m, w_in, b_in, conv_a_w, conv_a_b, a_log, d_skip, norm_a_w, conv_b_w, conv_b_b, norm_b_w, w_proj_a, w_proj_b, w_out, ln1_g, ln1_b, w_router, b_router, w_gu, b_gu, w_down, b_down, ln2_g, ln2_b):
    assert w_in.shape[0] == 1, "single layer"
    bsz, seq, _ = x_prompt.shape
    dbsz, dseq, _ = x_sample.shape
    assert seq % CHUNK == 0 and CONV_K - 1 <= dseq <= CHUNK
    w = _prep_weights(w_in[0], b_in[0], conv_a_w[0], conv_a_b[0], a_log[0], d_skip[0], norm_a_w[0],
                      conv_b_w[0], conv_b_b[0], norm_b_w[0], w_proj_a[0], w_proj_b[0], w_out[0],
                      ln1_g[0], ln1_b[0], w_router[0], b_router[0])

    zero_states = (jnp.zeros((bsz, CONV_K - 1, SSM_XBC), F32),
                   jnp.zeros((bsz, SSM_H, SSM_P, SSM_N), F32),
                   jnp.zeros((bsz, CONV_K - 1, 2 * ML_INNER), F32),
                   jnp.zeros((bsz, ML_H, ML_D, ML_D), F32),
                   jnp.zeros((bsz, ML_H, ML_D), F32),
                   jnp.zeros((bsz, ML_H), F32))
    past = (state_conv_a[0], state_ssm[0], state_conv_b[0], state_mlstm_c[0], state_mlstm_n[0], state_mlstm_m[0])
    xs_pad = jnp.pad(x_sample, ((0, 0), (0, CHUNK - dseq), (0, 0)))

    h_p, ti_p, tw_p, st_p = _token_mix(x_prompt, zero_states, w, CHUNK)
    h_s, ti_s, tw_s, st_s = _token_mix(xs_pad, past, w, dseq)

    n_p, n_s = h_p.shape[0], h_s.shape[0]
    n_tok = n_p + n_s
    top_i = jnp.concatenate([ti_p, ti_s], axis=0)
    top_w = jnp.concatenate([tw_p, tw_s], axis=0)
    block_e, used, row_tok, dst_prev, row_dst, row_w, n_rows_out = _routing_plan(top_i, top_w, n_tok)
    out2 = _moe(jnp.concatenate([h_p, h_s], axis=0), block_e, used, row_tok, dst_prev, row_dst, row_w,
                w_gu[0], b_gu[0][:, None, :], w_down[0], b_down[0][:, None, :], n_rows_out)
    g2, b2 = ln2_g[0].reshape(1, -1), ln2_b[0].reshape(1, -1)
    y_p = _ln2(h_p, out2, g2, b2, 0, n_tok).reshape(bsz, seq, D_MODEL)
    y_s = _ln2(h_s, out2, g2, b2, n_p, n_tok).reshape(dbsz, CHUNK, D_MODEL)[:, :dseq]
    return (y_p, y_s) + tuple(s[None] for s in st_p) + tuple(s[None] for s in st_s)
```
